```python
import jax, jax.numpy as jnp
from jax import lax
import numpy as np

D_MODEL = 1024
BATCH = 2
SEQ = 8192
DEPTH = 2
DEC_BATCH = 32
DEC_SEQ = 4
PAST_LEN = 16384
PAGE_SIZE = 128

N_HEADS = 8
HEAD_DIM = D_MODEL // 16
D_ATTN = N_HEADS * HEAD_DIM
D_SC = D_MODEL // 4
SC_WIDTH = 3
D_CC = D_MODEL // 4
CC_WIDTH = 31
D_FF = ((8 * D_MODEL // 3 + 127) // 128) * 128
N_BRANCH = 3
Q_BLOCK = 128
RMS_EPS = 1e-6
LN_EPS = 1e-5
FORGET_BIAS_INIT = 3.0

SPLIT_SIZES = (D_ATTN, D_ATTN, D_ATTN, N_HEADS, D_SC, D_SC, D_SC, D_CC, D_CC, N_BRANCH * D_MODEL)
SPLIT_POINTS = tuple(int(s) for s in np.cumsum(SPLIT_SIZES)[:-1])
D_IN = int(sum(SPLIT_SIZES))

kernel_name = "fox_shortconv_conformer_gated_hybrid_step"


def rmsnorm(x, g):
    xf = x.astype(jnp.float32)
    y = xf * lax.rsqrt(jnp.mean(xf * xf, axis=-1, keepdims=True) + RMS_EPS)
    return (y * g.astype(jnp.float32)).astype(x.dtype)


def layernorm(x, g, b):
    xf = x.astype(jnp.float32)
    mu = jnp.mean(xf, axis=-1, keepdims=True)
    var = jnp.mean(jnp.square(xf - mu), axis=-1, keepdims=True)
    y = (xf - mu) * lax.rsqrt(var + LN_EPS) * g.astype(jnp.float32) + b.astype(jnp.float32)
    return y.astype(x.dtype)


def swiglu(h, w1, w3, w2):
    return (jax.nn.silu(h @ w1) * (h @ w3)) @ w2


def causal_dwconv(u, prev, w):
    full = jnp.concatenate([prev.astype(u.dtype), u], axis=1)
    out = lax.conv_general_dilated(full, w[:, None, :].astype(u.dtype), window_strides=(1,),
                                   padding='VALID', dimension_numbers=('NWC', 'WIO', 'NWC'),
                                   feature_group_count=u.shape[-1])
    return out, full[:, full.shape[1] - (w.shape[0] - 1):]


def fox_attend(q, k, v, cq, ck, q_pos, k_pos):
    logits = jnp.einsum('bqhd,bkhd->bhqk', q, k).astype(jnp.float32) * (HEAD_DIM ** -0.5)
    logits = logits + jnp.transpose(cq, (0, 2, 1))[:, :, :, None] - jnp.transpose(ck, (0, 2, 1))[:, :, None, :]
    mask = k_pos[None, :] <= q_pos[:, None]
    logits = jnp.where(mask[None, None], logits, -jnp.inf)
    p = jax.nn.softmax(logits, axis=-1).astype(v.dtype)
    return jnp.einsum('bhqk,bkhd->bqhd', p, v)


def fox_prompt(q, k, v, c):
    B, T = q.shape[0], q.shape[1]
    nb = T // Q_BLOCK
    qb = q.reshape(B, nb, Q_BLOCK, N_HEADS, HEAD_DIM).swapaxes(0, 1)
    cb = c.reshape(B, nb, Q_BLOCK, N_HEADS).swapaxes(0, 1)
    k_pos = jnp.arange(T)

    def block(args):
        i, qi, ci = args
        q_pos = i * Q_BLOCK + jnp.arange(Q_BLOCK)
        return fox_attend(qi, k, v, ci, c, q_pos, k_pos)

    out = lax.map(block, (jnp.arange(nb), qb, cb))
    return out.swapaxes(0, 1).reshape(B, T, N_HEADS, HEAD_DIM)


def mixer(h, W, attn_fn, sc_prev, cc_prev):
    B, T = h.shape[0], h.shape[1]
    z = h @ W['w_in']
    q, k, v, fg, sb, sc, sx, ga, gb, gates = jnp.split(z, SPLIT_POINTS, axis=-1)
    q = q.reshape(B, T, N_HEADS, HEAD_DIM)
    k = k.reshape(B, T, N_HEADS, HEAD_DIM)
    v = v.reshape(B, T, N_HEADS, HEAD_DIM)
    logf = jax.nn.log_sigmoid((fg + W['b_f']).astype(jnp.float32))
    o_attn = attn_fn(q, k, v, logf).reshape(B, T, D_ATTN)
    sc_out, sc_state = causal_dwconv(sc * sx, sc_prev, W['conv_sc'])
    y_sc = sb * sc_out
    glu = ga * jax.nn.sigmoid(gb)
    dw, cc_state = causal_dwconv(glu, cc_prev, W['conv_cc'])
    y_cc = jax.nn.silu(layernorm(dw + W['b_cc'].astype(dw.dtype), W['ln_g_cc'], W['ln_b_cc']))
    g = jax.nn.sigmoid(gates).reshape(B, T, N_BRANCH, D_MODEL)
    merged = (g[:, :, 0] * (o_attn @ W['p_attn']) + g[:, :, 1] * (y_sc @ W['p_sc'])
              + g[:, :, 2] * (y_cc @ W['p_cc']))
    return merged @ W['w_o'], (k, v, logf, sc_state, cc_state)


def trunk_layer(x, W, attn_fn, sc_prev, cc_prev):
    x = x + 0.5 * swiglu(rmsnorm(x, W['g_ffn1']), W['w1_a'], W['w3_a'], W['w2_a'])
    mix, state = mixer(rmsnorm(x, W['g_mix']), W, attn_fn, sc_prev, cc_prev)
    x = x + mix
    x = x + 0.5 * swiglu(rmsnorm(x, W['g_ffn2']), W['w1_b'], W['w3_b'], W['w2_b'])
    return x, state


def setup_inputs(seed: int = 0) -> dict:
    key = jax.random.key(seed)
    ks = jax.random.split(key, 40)
    n_pages = PAST_LEN // PAGE_SIZE
    n_used = DEC_BATCH * n_pages
    n_pool = (5 * n_used + 3) // 4
    f32 = jnp.float32

    def nrm(k, shape, scale):
        return jax.random.normal(k, shape, f32) * scale

    def gain(k, shape):
        return 1.0 + 0.05 * jax.random.normal(k, shape, f32)

    page_table = jax.random.permutation(ks[0], n_pool)[:n_used].reshape(DEC_BATCH, n_pages).astype(jnp.int32)
    return {
        "x_prompt": nrm(ks[1], (BATCH, SEQ, D_MODEL), 1.0),
        "x_sample": nrm(ks[2], (DEC_BATCH, DEC_SEQ, D_MODEL), 1.0),
        "cache_k": nrm(ks[3], (DEPTH, n_pool, PAGE_SIZE, N_HEADS, HEAD_DIM), 1.0),
        "cache_v": nrm(ks[4], (DEPTH, n_pool, PAGE_SIZE, N_HEADS, HEAD_DIM), 1.0),
        "cache_logf": jax.nn.log_sigmoid(FORGET_BIAS_INIT + jax.random.normal(ks[5], (DEPTH, n_pool, PAGE_SIZE, N_HEADS), f32)),
        "state_sc": nrm(ks[6], (DEPTH, DEC_BATCH, SC_WIDTH - 1, D_SC), 1.0),
        "state_cc": nrm(ks[7], (DEPTH, DEC_BATCH, CC_WIDTH - 1, D_CC), 0.5),
        "page_table": page_table,
        "g_ffn1": gain(ks[8], (DEPTH, D_MODEL)),
        "w1_a": nrm(ks[9], (DEPTH, D_MODEL, D_FF), D_MODEL ** -0.5),
        "w3_a": nrm(ks[10], (DEPTH, D_MODEL, D_FF), D_MODEL ** -0.5),
        "w2_a": nrm(ks[11], (DEPTH, D_FF, D_MODEL), D_FF ** -0.5),
        "g_mix": gain(ks[12], (DEPTH, D_MODEL)),
        "w_in": nrm(ks[13], (DEPTH, D_MODEL, D_IN), D_MODEL ** -0.5),
        "b_f": FORGET_BIAS_INIT + 0.1 * jax.random.normal(ks[14], (DEPTH, N_HEADS), f32),
        "conv_sc": nrm(ks[15], (DEPTH, SC_WIDTH, D_SC), SC_WIDTH ** -0.5),
        "conv_cc": nrm(ks[16], (DEPTH, CC_WIDTH, D_CC), CC_WIDTH ** -0.5),
        "b_cc": nrm(ks[17], (DEPTH, D_CC), 0.01),
        "ln_g_cc": gain(ks[18], (DEPTH, D_CC)),
        "ln_b_cc": nrm(ks[19], (DEPTH, D_CC), 0.01),
        "p_attn": nrm(ks[20], (DEPTH, D_ATTN, D_MODEL), D_ATTN ** -0.5),
        "p_sc": nrm(ks[21], (DEPTH, D_SC, D_MODEL), D_SC ** -0.5),
        "p_cc": nrm(ks[22], (DEPTH, D_CC, D_MODEL), D_CC ** -0.5),
        "w_o": nrm(ks[23], (DEPTH, D_MODEL, D_MODEL), D_MODEL ** -0.5),
        "g_ffn2": gain(ks[24], (DEPTH, D_MODEL)),
        "w1_b": nrm(ks[25], (DEPTH, D_MODEL, D_FF), D_MODEL ** -0.5),
        "w3_b": nrm(ks[26], (DEPTH, D_MODEL, D_FF), D_MODEL ** -0.5),
        "w2_b": nrm(ks[27], (DEPTH, D_FF, D_MODEL), D_FF ** -0.5),
        "g_final": gain(ks[28], (D_MODEL,)),
    }


def reference(x_prompt, x_sample, cache_k, cache_v, cache_logf, state_sc, state_cc, page_table,
              g_ffn1, w1_a, w3_a, w2_a, g_mix, w_in, b_f, conv_sc, conv_cc, b_cc, ln_g_cc, ln_b_cc,
              p_attn, p_sc, p_cc, w_o, g_ffn2, w1_b, w3_b, w2_b, g_final):
    n_seq, n_pages = page_table.shape
    page = cache_k.shape[2]
    past = n_pages * page
    dec_seq = x_sample.shape[1]
    bp = x_prompt.shape[0]

    yp, ys = x_prompt, x_sample
    kp_l, vp_l, lfp_l, scp_l, ccp_l = [], [], [], [], []
    ks_l, vs_l, lfs_l, scs_l, ccs_l = [], [], [], [], []
    for l in range(DEPTH):
        W = {
            'g_ffn1': g_ffn1[l], 'w1_a': w1_a[l], 'w3_a': w3_a[l], 'w2_a': w2_a[l],
            'g_mix': g_mix[l], 'w_in': w_in[l], 'b_f': b_f[l], 'conv_sc': conv_sc[l],
            'conv_cc': conv_cc[l], 'b_cc': b_cc[l], 'ln_g_cc': ln_g_cc[l], 'ln_b_cc': ln_b_cc[l],
            'p_attn': p_attn[l], 'p_sc': p_sc[l], 'p_cc': p_cc[l], 'w_o': w_o[l],
            'g_ffn2': g_ffn2[l], 'w1_b': w1_b[l], 'w3_b': w3_b[l], 'w2_b': w2_b[l],
        }

        def attn_prompt(q, k, v, logf):
            return fox_prompt(q, k, v, jnp.cumsum(logf, axis=1))

        def attn_sample(q, k, v, logf, l=l):
            k_past = cache_k[l, page_table].reshape(n_seq, past, N_HEADS, HEAD_DIM)
            v_past = cache_v[l, page_table].reshape(n_seq, past, N_HEADS, HEAD_DIM)
            lf_past = cache_logf[l, page_table].reshape(n_seq, past, N_HEADS).astype(jnp.float32)
            k_all = jnp.concatenate([k_past.astype(k.dtype), k], axis=1)
            v_all = jnp.concatenate([v_past.astype(v.dtype), v], axis=1)
            c = jnp.cumsum(jnp.concatenate([lf_past, logf], axis=1), axis=1)
            q_pos = past + jnp.arange(dec_seq)
            k_pos = jnp.arange(past + dec_seq)
            return fox_attend(q, k_all, v_all, c[:, past:], c, q_pos, k_pos)

        sc0 = jnp.zeros((bp, SC_WIDTH - 1, D_SC), yp.dtype)
        cc0 = jnp.zeros((bp, CC_WIDTH - 1, D_CC), yp.dtype)
        yp, (k1, v1, lf1, sc1, cc1) = trunk_layer(yp, W, attn_prompt, sc0, cc0)
        ys, (k2, v2, lf2, sc2, cc2) = trunk_layer(ys, W, attn_sample, state_sc[l], state_cc[l])
        kp_l.append(k1); vp_l.append(v1); lfp_l.append(lf1); scp_l.append(sc1); ccp_l.append(cc1)
        ks_l.append(k2); vs_l.append(v2); lfs_l.append(lf2); scs_l.append(sc2); ccs_l.append(cc2)

    y_prompt = rmsnorm(yp, g_final)
    y_sample = rmsnorm(ys, g_final)
    return (y_prompt, y_sample,
            jnp.stack(kp_l), jnp.stack(vp_l), jnp.stack(lfp_l), jnp.stack(scp_l), jnp.stack(ccp_l),
            jnp.stack(ks_l), jnp.stack(vs_l), jnp.stack(lfs_l), jnp.stack(scs_l), jnp.stack(ccs_l))
```

```python
import functools

import jax
import jax.numpy as jnp
from jax import lax
from jax.experimental import pallas as pl
from jax.experimental.pallas import tpu as pltpu

RMS_EPS = 1e-6
LN_EPS = 1e-5
BF = jnp.bfloat16
F32 = jnp.float32

LANES = 128
SUBLANES = 8
VMEM_LIMIT = 56 * 1024 * 1024
NT_DIMS = (((1,), (1,)), ((), ()))


def _params(*sem):
    return pltpu.CompilerParams(dimension_semantics=sem, vmem_limit_bytes=VMEM_LIMIT)


def _pick(n, prefs):
    for t in prefs:
        if n % t == 0:
            return t
    return n


def _rms(x, g):
    return x * lax.rsqrt(jnp.mean(x * x, axis=-1, keepdims=True) + RMS_EPS) * g


def _dot(a, b):
    return jnp.dot(a, b, preferred_element_type=F32)


def _split3(x):
    hi = x.astype(BF)
    r1 = x - hi.astype(F32)
    mid = r1.astype(BF)
    lo = (r1 - mid.astype(F32)).astype(BF)
    return hi, mid, lo


def _log_sigmoid(x):
    return jnp.minimum(x, 0.0) - jnp.log1p(jnp.exp(-jnp.abs(x)))


def _ffn_body(x_ref, g_ref, w1_ref, w3_ref, w2_ref, gf_ref, o_ref, h_ref, acc_ref, *, final):
    f = pl.program_id(1)

    @pl.when(f == 0)
    def _():
        h_ref[...] = _rms(x_ref[...], g_ref[...]).astype(BF)
        acc_ref[...] = jnp.zeros_like(acc_ref)

    h = h_ref[...]
    a = _dot(h, w1_ref[...])
    b = _dot(h, w3_ref[...])
    u = (a * jax.nn.sigmoid(a)) * b
    acc_ref[...] += _dot(u.astype(BF), w2_ref[...])

    @pl.when(f == pl.num_programs(1) - 1)
    def _():
        y = x_ref[...] + 0.5 * acc_ref[...]
        if final:
            y = _rms(y, gf_ref[...])
        o_ref[...] = y


def _ffn(x, g, w1, w3, w2, g_final, *, final, tm):
    n, d = x.shape
    f = w1.shape[1]
    tf = _pick(f, (1408, 1024, 512, 256, 128))
    return pl.pallas_call(
        functools.partial(_ffn_body, final=final),
        grid=(n // tm, f // tf),
        in_specs=[
            pl.BlockSpec((tm, d), lambda i, j: (i, 0)),
            pl.BlockSpec((1, d), lambda i, j: (0, 0)),
            pl.BlockSpec((d, tf), lambda i, j: (0, j)),
            pl.BlockSpec((d, tf), lambda i, j: (0, j)),
            pl.BlockSpec((tf, d), lambda i, j: (j, 0)),
            pl.BlockSpec((1, d), lambda i, j: (0, 0)),
        ],
        out_specs=pl.BlockSpec((tm, d), lambda i, j: (i, 0)),
        out_shape=jax.ShapeDtypeStruct((n, d), F32),
        scratch_shapes=[pltpu.VMEM((tm, d), BF), pltpu.VMEM((tm, d), F32)],
        compiler_params=_params("parallel", "arbitrary"),
        name="ffn",
    )(x, g, w1, w3, w2, g_final)


def _dwconv(full_ref, w_ref, *, base, stride, width, r0, rows):
    acc = None
    for k in range(width):
        term = full_ref[pl.ds(base + k * stride + r0, rows), :] * w_ref[k:k + 1, :]
        acc = term if acc is None else acc + term
    return acc


def _mix_in_body(x_ref, g_ref, wqkv_ref, wfg_ref, bf_ref, wcv_ref, csc_ref, ccc_ref, bcc_ref,
                 lng_ref, lnb_ref, psc_ref, pcc_ref,
                 kf_ref, vf_ref, qb_ref, kb_ref, vb_ref, lf_ref, c_ref, ysc_ref, ycc_ref,
                 ssc_ref, scc_ref,
                 fsc_ref, fcc_ref, sb_ref, carry_ref,
                 *, tm, stride, da, dsc, dcc, wsc, wcc, nh, scale):
    i = pl.program_id(1)
    hsc, hcc = (wsc - 1) * stride, (wcc - 1) * stride
    psc, pcc = (-hsc) % SUBLANES, (-hcc) % SUBLANES

    h = _rms(x_ref[...], g_ref[...]).astype(BF)

    qkv = _dot(h, wqkv_ref[...])
    k = qkv[:, da:2 * da]
    v = qkv[:, 2 * da:3 * da]
    kf_ref[...] = k
    vf_ref[...] = v
    qb_ref[...] = (qkv[:, 0:da] * scale).astype(BF)
    kb_ref[...] = k.astype(BF)
    vb_ref[...] = v.astype(BF)

    lane = lax.broadcasted_iota(jnp.int32, (tm, LANES), 1)
    logf = jnp.where(lane < nh, _log_sigmoid(_dot(h, wfg_ref[...]) + bf_ref[...]), 0.0)
    lf_ref[...] = logf[:, 0:nh]
    row = lax.broadcasted_iota(jnp.int32, (tm, tm), 0)
    col = lax.broadcasted_iota(jnp.int32, (tm, tm), 1)
    tri = jnp.where(col <= row, 1.0, 0.0).astype(BF)
    hi, mid, lo = _split3(logf)

    @pl.when(i == 0)
    def _():
        carry_ref[...] = jnp.zeros_like(carry_ref)

    c = _dot(tri, hi) + _dot(tri, mid) + _dot(tri, lo) + carry_ref[...]
    carry_ref[...] = c[tm - 1:tm, :]
    c_ref[...] = c[:, 0:nh]

    cv = _dot(h, wcv_ref[...])
    sb_ref[...] = cv[:, 0:dsc]
    u_sc = cv[:, dsc:2 * dsc] * cv[:, 2 * dsc:3 * dsc]
    o = 3 * dsc
    glu = cv[:, o:o + dcc] * jax.nn.sigmoid(cv[:, o + dcc:o + 2 * dcc])

    @pl.when(i == 0)
    def _():
        fsc_ref[pl.ds(psc, hsc), :] = psc_ref[0]
        fcc_ref[pl.ds(pcc, hcc), :] = pcc_ref[0]

    @pl.when(i > 0)
    def _():
        fsc_ref[pl.ds(psc, hsc), :] = fsc_ref[pl.ds(psc + tm, hsc), :]
        fcc_ref[pl.ds(pcc, hcc), :] = fcc_ref[pl.ds(pcc + tm, hcc), :]

    fsc_ref[pl.ds(psc + hsc, tm), :] = u_sc
    fcc_ref[pl.ds(pcc + hcc, tm), :] = glu
    ssc_ref[0] = fsc_ref[pl.ds(psc + tm, hsc), :]
    scc_ref[0] = fcc_ref[pl.ds(pcc + tm, hcc), :]

    rc = min(tm, 128)
    for r0 in range(0, tm, rc):
        y = _dwconv(fsc_ref, csc_ref, base=psc, stride=stride, width=wsc, r0=r0, rows=rc)
        ysc_ref[pl.ds(r0, rc), :] = (sb_ref[pl.ds(r0, rc), :] * y).astype(BF)
        dw = _dwconv(fcc_ref, ccc_ref, base=pcc, stride=stride, width=wcc, r0=r0, rows=rc)
        dw = dw + bcc_ref[...]
        mu = jnp.mean(dw, axis=-1, keepdims=True)
        var = jnp.mean(jnp.square(dw - mu), axis=-1, keepdims=True)
        ln = (dw - mu) * lax.rsqrt(var + LN_EPS) * lng_ref[...] + lnb_ref[...]
        ycc_ref[pl.ds(r0, rc), :] = (ln * jax.nn.sigmoid(ln)).astype(BF)


def _mix_in(x, lw, prev_sc, prev_cc, *, groups, tm, stride):
    n, d = x.shape
    tpg = n // groups // tm
    da = lw["p_attn"].shape[0]
    dsc = lw["conv_sc"].shape[1]
    dcc = lw["conv_cc"].shape[1]
    wsc = lw["conv_sc"].shape[0]
    wcc = lw["conv_cc"].shape[0]
    nh = lw["nh"]
    hsc, hcc = (wsc - 1) * stride, (wcc - 1) * stride
    psc, pcc = (-hsc) % SUBLANES, (-hcc) % SUBLANES
    body = functools.partial(_mix_in_body, tm=tm, stride=stride, da=da, dsc=dsc, dcc=dcc,
                             wsc=wsc, wcc=wcc, nh=nh, scale=float(lw["hd"]) ** -0.5)
    rowspec = lambda w: pl.BlockSpec((tm, w), lambda g, i: (g * tpg + i, 0))
    const = lambda a: pl.BlockSpec(a.shape, lambda g, i: (0,) * a.ndim)
    grp = lambda r, c: pl.BlockSpec((1, r, c), lambda g, i: (g, 0, 0))
    small = [lw["conv_sc"], lw["conv_cc"], lw["b_cc"], lw["ln_g_cc"], lw["ln_b_cc"]]
    return pl.pallas_call(
        body,
        grid=(groups, tpg),
        in_specs=[rowspec(d), const(lw["g_mix"]), const(lw["w_qkv"]), const(lw["w_fg"]),
                  const(lw["b_f"]), const(lw["w_cv"])] + [const(a) for a in small]
                 + [grp(hsc, dsc), grp(hcc, dcc)],
        out_specs=[rowspec(da), rowspec(da), rowspec(da), rowspec(da), rowspec(da),
                   rowspec(nh), rowspec(nh), rowspec(dsc), rowspec(dcc),
                   grp(hsc, dsc), grp(hcc, dcc)],
        out_shape=[jax.ShapeDtypeStruct((n, da), F32), jax.ShapeDtypeStruct((n, da), F32),
                   jax.ShapeDtypeStruct((n, da), BF), jax.ShapeDtypeStruct((n, da), BF),
                   jax.ShapeDtypeStruct((n, da), BF),
                   jax.ShapeDtypeStruct((n, nh), F32), jax.ShapeDtypeStruct((n, nh), F32),
                   jax.ShapeDtypeStruct((n, dsc), BF), jax.ShapeDtypeStruct((n, dcc), BF),
                   jax.ShapeDtypeStruct((groups, hsc, dsc), F32),
                   jax.ShapeDtypeStruct((groups, hcc, dcc), F32)],
        scratch_shapes=[pltpu.VMEM((psc + hsc + tm, dsc), F32),
                        pltpu.VMEM((pcc + hcc + tm, dcc), F32),
                        pltpu.VMEM((tm, dsc), F32),
                        pltpu.VMEM((1, LANES), F32)],
        compiler_params=_params("arbitrary", "arbitrary"),
        name="mix_in",
    )(x, lw["g_mix"], lw["w_qkv"], lw["w_fg"], lw["b_f"], lw["w_cv"], *small, prev_sc, prev_cc)


def _mix_out_body(x_ref, o_ref, ysc_ref, ycc_ref, g_ref, wg_ref, pa_ref, ps_ref, pc_ref, wo_ref,
                  y_ref, *, d):
    x = x_ref[...]
    h = _rms(x, g_ref[...]).astype(BF)
    merged = None
    for j, (b_ref, p_ref) in enumerate(((o_ref, pa_ref), (ysc_ref, ps_ref), (ycc_ref, pc_ref))):
        gate = jax.nn.sigmoid(_dot(h, wg_ref[:, j * d:(j + 1) * d]))
        term = gate * _dot(b_ref[...], p_ref[...])
        merged = term if merged is None else merged + term
    y_ref[...] = x + _dot(merged.astype(BF), wo_ref[...])


def _mix_out(x, o_attn, ysc, ycc, lw, *, tm):
    n, d = x.shape
    rowspec = lambda a: pl.BlockSpec((tm, a.shape[1]), lambda i: (i, 0))
    const = lambda a: pl.BlockSpec(a.shape, lambda i: (0,) * a.ndim)
    ws = [lw["g_mix"], lw["w_gates"], lw["p_attn"], lw["p_sc"], lw["p_cc"], lw["w_o"]]
    return pl.pallas_call(
        functools.partial(_mix_out_body, d=d),
        grid=(n // tm,),
        in_specs=[rowspec(x), rowspec(o_attn), rowspec(ysc), rowspec(ycc)] + [const(a) for a in ws],
        out_specs=rowspec(x),
        out_shape=jax.ShapeDtypeStruct((n, d), F32),
        compiler_params=_params("parallel"),
        name="mix_out",
    )(x, o_attn, ysc, ycc, *ws)


def _flash_body(q_ref, k_ref, v_ref, cq_ref, ck_ref, o_ref, m_ref, l_ref, acc_ref, *, tq, hd):
    qi = pl.program_id(2)
    ki = pl.program_id(3)

    @pl.when(ki == 0)
    def _():
        m_ref[...] = jnp.full_like(m_ref, -jnp.inf)
        l_ref[...] = jnp.zeros_like(l_ref)
        acc_ref[...] = jnp.zeros_like(acc_ref)

    def step(diagonal):
        q = q_ref[0]
        k = k_ref[0]
        v = v_ref[0]
        lane = lax.broadcasted_iota(jnp.int32, q.shape, 1)
        for hh in range(LANES // hd):
            qh = jnp.where((lane >= hh * hd) & (lane < (hh + 1) * hd), q, jnp.zeros_like(q))
            s = lax.dot_general(qh, k, NT_DIMS, preferred_element_type=F32)
            s = s + cq_ref[0, hh] - ck_ref[0, hh]
            if diagonal:
                row = lax.broadcasted_iota(jnp.int32, s.shape, 0)
                col = lax.broadcasted_iota(jnp.int32, s.shape, 1)
                s = jnp.where(col <= row, s, -jnp.inf)
            m_prev = m_ref[hh]
            m_new = jnp.maximum(m_prev, jnp.max(s, axis=-1, keepdims=True))
            alpha = jnp.exp(m_prev - m_new)
            p = jnp.exp(s - m_new)
            l_ref[hh] = alpha * l_ref[hh] + jnp.sum(p, axis=-1, keepdims=True)
            acc_ref[hh] = alpha * acc_ref[hh] + _dot(p.astype(BF), v)
            m_ref[hh] = m_new

    @pl.when(ki < qi)
    def _():
        step(False)

    @pl.when(ki == qi)
    def _():
        step(True)
        lane = lax.broadcasted_iota(jnp.int32, (tq, LANES), 1)
        out = None
        for hh in range(LANES // hd):
            o = acc_ref[hh] / l_ref[hh]
            out = o if out is None else jnp.where(lane >= hh * hd, o, out)
        o_ref[0] = out.astype(BF)


def _flash_prompt(q, k, v, cq, ck, *, hd, tq):
    b, t, da = q.shape
    hpb = LANES // hd
    nq = t // tq
    return pl.pallas_call(
        functools.partial(_flash_body, tq=tq, hd=hd),
        grid=(b, da // LANES, nq, nq),
        in_specs=[
            pl.BlockSpec((1, tq, LANES), lambda bb, hp, qi, ki: (bb, qi, hp)),
            pl.BlockSpec((1, tq, LANES), lambda bb, hp, qi, ki: (bb, jnp.minimum(ki, qi), hp)),
            pl.BlockSpec((1, tq, LANES), lambda bb, hp, qi, ki: (bb, jnp.minimum(ki, qi), hp)),
            pl.BlockSpec((1, hpb, tq, 1), lambda bb, hp, qi, ki: (bb, hp, qi, 0)),
            pl.BlockSpec((1, hpb, 1, tq), lambda bb, hp, qi, ki: (bb, hp, 0, jnp.minimum(ki, qi))),
        ],
        out_specs=pl.BlockSpec((1, tq, LANES), lambda bb, hp, qi, ki: (bb, qi, hp)),
        out_shape=jax.ShapeDtypeStruct((b, t, da), BF),
        scratch_shapes=[pltpu.VMEM((hpb, tq, 1), F32), pltpu.VMEM((hpb, tq, 1), F32),
                        pltpu.VMEM((hpb, tq, LANES), F32)],
        compiler_params=_params("parallel", "parallel", "parallel", "arbitrary"),
        name="flash_prompt",
    )(q, k, v, cq, ck)


def _sattn_body(pt_ref, qe_ref, kn_ref, vn_ref, lfn_ref, *rest, npg, nh, hd, page, nq):
    k_refs, v_refs, lf_refs = rest[0:npg], rest[npg:2 * npg], rest[2 * npg:3 * npg]
    o_ref = rest[3 * npg]
    m_ref, l_ref, acc_ref, carry_ref, sq_ref = rest[3 * npg + 1:]
    c = pl.program_id(1)
    rows = nq * nh
    da = nh * hd

    rowi = lax.broadcasted_iota(jnp.int32, (rows, da), 0)
    lanei = lax.broadcasted_iota(jnp.int32, (rows, da), 1)
    head_cols = (lanei // hd) == (rowi % nh)
    qm = jnp.where(head_cols, qe_ref[0], jnp.zeros((rows, da), BF))

    j_i = lax.broadcasted_iota(jnp.int32, (page, page), 0)
    s_i = lax.broadcasted_iota(jnp.int32, (page, page), 1)
    later = jnp.where(j_i > s_i, 1.0, 0.0).astype(BF)
    key_i = lax.broadcasted_iota(jnp.int32, (rows, page), 1)
    qidx = lax.broadcasted_iota(jnp.int32, (rows, page), 0) // nh

    def page_bias(lf, carry):
        hi, mid, lo = _split3(lf)
        suff = _dot(hi, later) + _dot(mid, later) + _dot(lo, later) + carry
        return jnp.concatenate([suff] * nq, axis=0), carry + jnp.sum(lf, axis=1, keepdims=True)

    def attend(s_list, vt_list):
        s = jnp.concatenate(s_list, axis=1) if len(s_list) > 1 else s_list[0]
        m_prev = m_ref[...]
        m_new = jnp.maximum(m_prev, jnp.max(s, axis=-1, keepdims=True))
        alpha = jnp.exp(m_prev - m_new)
        p = jnp.exp(s - m_new)
        l_ref[...] = alpha * l_ref[...] + jnp.sum(p, axis=-1, keepdims=True)
        pv = None
        for j, vt in enumerate(vt_list):
            t = lax.dot_general(p[:, j * page:(j + 1) * page].astype(BF), vt, NT_DIMS,
                                preferred_element_type=F32)
            pv = t if pv is None else pv + t
        acc_ref[...] = alpha * acc_ref[...] + pv
        m_ref[...] = m_new

    @pl.when(c == 0)
    def _():
        m_ref[...] = jnp.full_like(m_ref, -jnp.inf)
        l_ref[...] = jnp.zeros_like(l_ref)
        acc_ref[...] = jnp.zeros_like(acc_ref)
        bias, carry = page_bias(lfn_ref[0], jnp.zeros((nh, 1), F32))
        sq = jnp.sum(jnp.where(key_i == qidx, bias, 0.0), axis=-1, keepdims=True)
        sq_ref[...] = sq
        carry_ref[...] = carry
        s = _dot(qm, kn_ref[0].astype(BF))
        s = jnp.where(key_i <= qidx, s + bias - sq, -jnp.inf)
        attend([s], [vn_ref[0].astype(BF)])

    carry = carry_ref[...]
    sq = sq_ref[...]
    s_list, vt_list = [], []
    for j in range(npg):
        bias, carry = page_bias(lf_refs[j][0, 0], carry)
        s_list.append(_dot(qm, k_refs[j][0, 0].astype(BF)) + bias - sq)
        vt_list.append(v_refs[j][0, 0].astype(BF))
    carry_ref[...] = carry
    attend(s_list, vt_list)

    @pl.when(c == pl.num_programs(1) - 1)
    def _():
        o = jnp.where(head_cols, acc_ref[...] / l_ref[...], 0.0)
        o_ref[0] = jnp.sum(o.reshape(nq, nh, da), axis=1).astype(BF)


def _sample_attn(layer, page_table, qe, knt, vnt, lfnt, cache_kt, cache_vt, cache_lft, *, nh, hd, nq):
    nseq, n_pages = page_table.shape
    page = cache_kt.shape[3]
    da = nh * hd
    npg = _pick(n_pages, (8, 4, 2, 1))
    n_chunks = n_pages // npg
    rows = nq * nh

    def pg(j):
        return lambda b, c, pt: (layer, pt[b, n_pages - 1 - (c * npg + j)], 0, 0)

    seq3 = lambda r, w: pl.BlockSpec((1, r, w), lambda b, c, pt: (b, 0, 0))
    in_specs = [seq3(rows, da), seq3(da, page), seq3(da, page), seq3(nh, page)]
    in_specs += [pl.BlockSpec((1, 1, da, page), pg(j)) for j in range(npg)]
    in_specs += [pl.BlockSpec((1, 1, da, page), pg(j)) for j in range(npg)]
    in_specs += [pl.BlockSpec((1, 1, nh, page), pg(j)) for j in range(npg)]
    grid_spec = pltpu.PrefetchScalarGridSpec(
        num_scalar_prefetch=1,
        grid=(nseq, n_chunks),
        in_specs=in_specs,
        out_specs=pl.BlockSpec((1, nq, da), lambda b, c, pt: (b, 0, 0)),
        scratch_shapes=[pltpu.VMEM((rows, 1), F32), pltpu.VMEM((rows, 1), F32),
                        pltpu.VMEM((rows, da), F32), pltpu.VMEM((nh, 1), F32),
                        pltpu.VMEM((rows, 1), F32)],
    )
    return pl.pallas_call(
        functools.partial(_sattn_body, npg=npg, nh=nh, hd=hd, page=page, nq=nq),
        grid_spec=grid_spec,
        out_shape=jax.ShapeDtypeStruct((nseq, nq, da), BF),
        compiler_params=_params("arbitrary", "arbitrary"),
        name="sample_attn",
    )(page_table, qe, knt, vnt, lfnt, *([cache_kt] * npg), *([cache_vt] * npg),
      *([cache_lft] * npg))


def _layer_weights(l, nh, hd, g_ffn1, w1_a, w3_a, w2_a, g_mix, w_in, b_f, conv_sc, conv_cc, b_cc,
                   ln_g_cc, ln_b_cc, p_attn, p_sc, p_cc, w_o, g_ffn2, w1_b, w3_b, w2_b):
    da = nh * hd
    dsc = conv_sc.shape[2]
    dcc = conv_cc.shape[2]
    d = w_in.shape[1]
    win = w_in[l]
    o_fg = 3 * da
    o_cv = o_fg + nh
    o_g = o_cv + 3 * dsc + 2 * dcc
    row = lambda a: a[l].reshape(1, -1)
    return dict(
        nh=nh, hd=hd,
        g_ffn1=row(g_ffn1), w1_a=w1_a[l].astype(BF), w3_a=w3_a[l].astype(BF), w2_a=w2_a[l].astype(BF),
        g_ffn2=row(g_ffn2), w1_b=w1_b[l].astype(BF), w3_b=w3_b[l].astype(BF), w2_b=w2_b[l].astype(BF),
        g_mix=row(g_mix),
        w_qkv=win[:, 0:o_fg].astype(BF),
        w_fg=jnp.pad(win[:, o_fg:o_cv], ((0, 0), (0, LANES - nh))).astype(BF),
        b_f=jnp.pad(b_f[l], (0, LANES - nh)).reshape(1, LANES),
        w_cv=win[:, o_cv:o_g].astype(BF),
        w_gates=win[:, o_g:].astype(BF),
        conv_sc=conv_sc[l], conv_cc=conv_cc[l], b_cc=row(b_cc), ln_g_cc=row(ln_g_cc),
        ln_b_cc=row(ln_b_cc),
        p_attn=p_attn[l].astype(BF), p_sc=p_sc[l].astype(BF), p_cc=p_cc[l].astype(BF),
        w_o=w_o[l].astype(BF),
    )


def kernel(x_prompt, x_sample, cache_k, cache_v, cache_logf, state_sc, state_cc, page_table, g_ffn1, w1_a, w3_a, w2_a, g_mix, w_in, b_f, conv_sc, conv_cc, b_cc, ln_g_cc, ln_b_cc, p_attn, p_sc, p_cc, w_o, g_ffn2, w1_b, w3_b, w2_b, g_final):
    bp, t, d = x_prompt.shape
    bs, ts, _ = x_sample.shape
    depth = w_in.shape[0]
    nh = b_f.shape[1]
    da = p_attn.shape[1]
    hd = da // nh
    dsc, dcc = conv_sc.shape[2], conv_cc.shape[2]
    wsc, wcc = conv_sc.shape[1], conv_cc.shape[1]
    n_pool, page = cache_k.shape[1], cache_k.shape[2]
    assert LANES % hd == 0 and da % LANES == 0 and bs % SUBLANES == 0 and ts <= page

    tm_p = _pick(t, (512, 256, 128))
    tq = _pick(t, (512, 256, 128))
    ns = bs * ts
    gfin = g_final.reshape(1, d)

    ckt = jnp.transpose(cache_k, (0, 1, 3, 4, 2)).reshape(depth, n_pool, da, page)
    cvt = jnp.transpose(cache_v, (0, 1, 3, 4, 2)).reshape(depth, n_pool, da, page)
    clft = jnp.swapaxes(cache_logf, 2, 3)

    yp = x_prompt.reshape(bp * t, d)
    ys = jnp.swapaxes(x_sample, 0, 1).reshape(ns, d)
    zero_sc = jnp.zeros((bp, wsc - 1, dsc), F32)
    zero_cc = jnp.zeros((bp, wcc - 1, dcc), F32)

    outs = [[] for _ in range(10)]
    for l in range(depth):
        lw = _layer_weights(l, nh, hd, g_ffn1, w1_a, w3_a, w2_a, g_mix, w_in, b_f, conv_sc, conv_cc,
                            b_cc, ln_g_cc, ln_b_cc, p_attn, p_sc, p_cc, w_o, g_ffn2, w1_b, w3_b, w2_b)
        last = l == depth - 1

        yp = _ffn(yp, lw["g_ffn1"], lw["w1_a"], lw["w3_a"], lw["w2_a"], gfin, final=False, tm=tm_p)
        kf, vf, qb, kb, vb, lf, c, ysc, ycc, ssc, scc = _mix_in(
            yp, lw, zero_sc, zero_cc, groups=bp, tm=tm_p, stride=1)
        c3 = c.reshape(bp, t, nh)
        cq = jnp.swapaxes(c3, 1, 2)[..., None]
        ck = jnp.swapaxes(c3, 1, 2)[:, :, None, :]
        o = _flash_prompt(qb.reshape(bp, t, da), kb.reshape(bp, t, da), vb.reshape(bp, t, da),
                          cq, ck, hd=hd, tq=tq)
        yp = _mix_out(yp, o.reshape(bp * t, da), ysc, ycc, lw, tm=tm_p)
        yp = _ffn(yp, lw["g_ffn2"], lw["w1_b"], lw["w3_b"], lw["w2_b"], gfin, final=last, tm=tm_p)
        outs[0].append(kf.reshape(bp, t, nh, hd))
        outs[1].append(vf.reshape(bp, t, nh, hd))
        outs[2].append(lf.reshape(bp, t, nh))
        outs[3].append(ssc)
        outs[4].append(scc)

        ys = _ffn(ys, lw["g_ffn1"], lw["w1_a"], lw["w3_a"], lw["w2_a"], gfin, final=False, tm=ns)
        psc = jnp.swapaxes(state_sc[l], 0, 1).reshape(1, (wsc - 1) * bs, dsc)
        pcc = jnp.swapaxes(state_cc[l], 0, 1).reshape(1, (wcc - 1) * bs, dcc)
        kf, vf, qb, kb, vb, lf, c, ysc, ycc, ssc, scc = _mix_in(
            ys, lw, psc, pcc, groups=1, tm=ns, stride=bs)
        seq_major = lambda a: jnp.swapaxes(a.reshape(ts, bs, -1), 0, 1)
        featmaj = lambda a: jnp.pad(jnp.swapaxes(seq_major(a), 1, 2),
                                    ((0, 0), (0, 0), (0, page - ts)))
        qe = jnp.repeat(seq_major(qb), nh, axis=1)
        o = _sample_attn(l, page_table, qe, featmaj(kf), featmaj(vf), featmaj(lf), ckt, cvt, clft,
                         nh=nh, hd=hd, nq=ts)
        o = jnp.swapaxes(o, 0, 1).reshape(ns, da)
        ys = _mix_out(ys, o, ysc, ycc, lw, tm=ns)
        ys = _ffn(ys, lw["g_ffn2"], lw["w1_b"], lw["w3_b"], lw["w2_b"], gfin, final=last, tm=ns)
        outs[5].append(seq_major(kf).reshape(bs, ts, nh, hd))
        outs[6].append(seq_major(vf).reshape(bs, ts, nh, hd))
        outs[7].append(seq_major(lf))
        outs[8].append(jnp.swapaxes(ssc.reshape(wsc - 1, bs, dsc), 0, 1))
        outs[9].append(jnp.swapaxes(scc.reshape(wcc - 1, bs, dcc), 0, 1))

    y_prompt = yp.reshape(bp, t, d)
    y_sample = jnp.swapaxes(ys.reshape(ts, bs, d), 0, 1)
    return (y_prompt, y_sample) + tuple(jnp.stack(o) for o in outs)
```

```python
import functools

import jax
import jax.numpy as jnp
from jax import lax
from jax.experimental import pallas as pl
from jax.experimental.pallas import tpu as pltpu

RMS_EPS = 1e-6
LN_EPS = 1e-5
BF = jnp.bfloat16
F32 = jnp.float32

LANES = 128
SUBLANES = 8
VMEM_LIMIT = 56 * 1024 * 1024
NT_DIMS = (((1,), (1,)), ((), ()))


def _params(*sem):
    return pltpu.CompilerParams(dimension_semantics=sem, vmem_limit_bytes=VMEM_LIMIT)


def _pick(n, prefs):
    for t in prefs:
        if n % t == 0:
            return t
    return n


def _rms(x, g):
    return x * lax.rsqrt(jnp.mean(x * x, axis=-1, keepdims=True) + RMS_EPS) * g


def _dot(a, b):
    return jnp.dot(a, b, preferred_element_type=F32)


def _split3(x):
    hi = x.astype(BF)
    r1 = x - hi.astype(F32)
    mid = r1.astype(BF)
    lo = (r1 - mid.astype(F32)).astype(BF)
    return hi, mid, lo


def _log_sigmoid(x):
    return jnp.minimum(x, 0.0) - jnp.log1p(jnp.exp(-jnp.abs(x)))


def _ffn_body(x_ref, g_ref, w1_ref, w3_ref, w2_ref, gf_ref, o_ref, h_ref, acc_ref, *, final):
    f = pl.program_id(1)

    @pl.when(f == 0)
    def _():
        h_ref[...] = _rms(x_ref[...], g_ref[...]).astype(BF)
        acc_ref[...] = jnp.zeros_like(acc_ref)

    h = h_ref[...]
    a = _dot(h, w1_ref[...])
    b = _dot(h, w3_ref[...])
    u = (a * jax.nn.sigmoid(a)) * b
    acc_ref[...] += _dot(u.astype(BF), w2_ref[...])

    @pl.when(f == pl.num_programs(1) - 1)
    def _():
        y = x_ref[...] + 0.5 * acc_ref[...]
        if final:
            y = _rms(y, gf_ref[...])
        o_ref[...] = y


def _ffn(x, g, w1, w3, w2, g_final, *, final, tm):
    n, d = x.shape
    f = w1.shape[1]
    tf = _pick(f, (1408, 1024, 512, 256, 128))
    return pl.pallas_call(
        functools.partial(_ffn_body, final=final),
        grid=(n // tm, f // tf),
        in_specs=[
            pl.BlockSpec((tm, d), lambda i, j: (i, 0)),
            pl.BlockSpec((1, d), lambda i, j: (0, 0)),
            pl.BlockSpec((d, tf), lambda i, j: (0, j)),
            pl.BlockSpec((d, tf), lambda i, j: (0, j)),
            pl.BlockSpec((tf, d), lambda i, j: (j, 0)),
            pl.BlockSpec((1, d), lambda i, j: (0, 0)),
        ],
        out_specs=pl.BlockSpec((tm, d), lambda i, j: (i, 0)),
        out_shape=jax.ShapeDtypeStruct((n, d), F32),
        scratch_shapes=[pltpu.VMEM((tm, d), BF), pltpu.VMEM((tm, d), F32)],
        compiler_params=_params("parallel", "arbitrary"),
        name="ffn",
    )(x, g, w1, w3, w2, g_final)


def _dwconv(full_ref, w_ref, *, base, stride, width, r0, rows):
    acc = None
    for k in range(width):
        term = full_ref[pl.ds(base + k * stride + r0, rows), :] * w_ref[k:k + 1, :]
        acc = term if acc is None else acc + term
    return acc


def _mix_in_body(x_ref, g_ref, wqt_ref, wkt_ref, wvt_ref, wka_ref, pk_ref, kc_ref, wfgt_ref, bft_ref,
                 wfg_ref, bf_ref, wcv_ref, csc_ref, ccc_ref, bcc_ref, lng_ref, lnb_ref, psc_ref, pcc_ref,
                 ktf_ref, vtf_ref, qta_ref, ka_ref, vtb_ref, lft_ref, ysc_ref, ycc_ref,
                 ssc_ref, scc_ref,
                 fsc_ref, fcc_ref, sb_ref, carry_ref, carryt_ref,
                 *, tm, stride, da, dsc, dcc, wsc, wcc, nh, hd):
    i = pl.program_id(1)
    hsc, hcc = (wsc - 1) * stride, (wcc - 1) * stride
    psc, pcc = (-hsc) % SUBLANES, (-hcc) % SUBLANES

    h = _rms(x_ref[...], g_ref[...]).astype(BF)
    nt = lambda w_ref: lax.dot_general(w_ref[...], h, NT_DIMS, preferred_element_type=F32)

    qt = nt(wqt_ref)
    kt = nt(wkt_ref)
    vt = nt(wvt_ref)
    ktf_ref[0] = kt
    vtf_ref[0] = vt
    vtb_ref[0] = vt.astype(BF)

    @pl.when(i == 0)
    def _():
        carry_ref[...] = jnp.zeros_like(carry_ref)
        carryt_ref[...] = jnp.zeros_like(carryt_ref)

    logft = _log_sigmoid(nt(wfgt_ref) + bft_ref[...])
    lft_ref[0] = logft
    row = lax.broadcasted_iota(jnp.int32, (tm, tm), 0)
    col = lax.broadcasted_iota(jnp.int32, (tm, tm), 1)
    upto = jnp.where(row <= col, 1.0, 0.0).astype(BF)
    hi, mid, lo = _split3(logft)
    ct = _dot(hi, upto) + _dot(mid, upto) + _dot(lo, upto) + carryt_ref[...]
    carryt_ref[...] = ct[:, tm - 1:tm]

    lane = lax.broadcasted_iota(jnp.int32, (tm, LANES), 1)
    logf = jnp.where(lane < nh, _log_sigmoid(_dot(h, wfg_ref[...]) + bf_ref[...]), 0.0)
    tri = jnp.where(col <= row, 1.0, 0.0).astype(BF)
    hi, mid, lo = _split3(logf)
    c = _dot(tri, hi) + _dot(tri, mid) + _dot(tri, lo) + carry_ref[...]
    carry_ref[...] = c[tm - 1:tm, :]

    chi, cmid, clo = _split3(c)
    ka = (_dot(h, wka_ref[...]) + _dot(chi, pk_ref[0]) + _dot(cmid, pk_ref[1]) + _dot(clo, pk_ref[2])
          + kc_ref[...])
    ka_ref[...] = ka.astype(BF)
    thi, tmid, tlo = _split3(ct)
    aug = jnp.concatenate([thi.astype(F32), tmid.astype(F32), tlo.astype(F32),
                           jnp.ones((SUBLANES, tm), F32),
                           jnp.zeros((LANES - hd - 4 * SUBLANES, tm), F32)], axis=0)
    for hh in range(nh):
        blk = jnp.concatenate([qt[hh * hd:(hh + 1) * hd, :], aug], axis=0)
        qta_ref[0, pl.ds(hh * LANES, LANES), :] = blk.astype(BF)

    cv = _dot(h, wcv_ref[...])
    sb_ref[...] = cv[:, 0:dsc]
    u_sc = cv[:, dsc:2 * dsc] * cv[:, 2 * dsc:3 * dsc]
    o = 3 * dsc
    glu = cv[:, o:o + dcc] * jax.nn.sigmoid(cv[:, o + dcc:o + 2 * dcc])

    @pl.when(i == 0)
    def _():
        fsc_ref[pl.ds(psc, hsc), :] = psc_ref[0]
        fcc_ref[pl.ds(pcc, hcc), :] = pcc_ref[0]

    @pl.when(i > 0)
    def _():
        fsc_ref[pl.ds(psc, hsc), :] = fsc_ref[pl.ds(psc + tm, hsc), :]
        fcc_ref[pl.ds(pcc, hcc), :] = fcc_ref[pl.ds(pcc + tm, hcc), :]

    fsc_ref[pl.ds(psc + hsc, tm), :] = u_sc
    fcc_ref[pl.ds(pcc + hcc, tm), :] = glu
    ssc_ref[0] = fsc_ref[pl.ds(psc + tm, hsc), :]
    scc_ref[0] = fcc_ref[pl.ds(pcc + tm, hcc), :]

    rc = min(tm, 128)
    for r0 in range(0, tm, rc):
        y = _dwconv(fsc_ref, csc_ref, base=psc, stride=stride, width=wsc, r0=r0, rows=rc)
        ysc_ref[pl.ds(r0, rc), :] = (sb_ref[pl.ds(r0, rc), :] * y).astype(BF)
        dw = _dwconv(fcc_ref, ccc_ref, base=pcc, stride=stride, width=wcc, r0=r0, rows=rc)
        dw = dw + bcc_ref[...]
        mu = jnp.mean(dw, axis=-1, keepdims=True)
        var = jnp.mean(jnp.square(dw - mu), axis=-1, keepdims=True)
        ln = (dw - mu) * lax.rsqrt(var + LN_EPS) * lng_ref[...] + lnb_ref[...]
        ycc_ref[pl.ds(r0, rc), :] = (ln * jax.nn.sigmoid(ln)).astype(BF)


def _mix_in(x, lw, prev_sc, prev_cc, *, groups, tm, stride):
    n, d = x.shape
    tpg = n // groups // tm
    rpg = n // groups
    da = lw["p_attn"].shape[0]
    dsc = lw["conv_sc"].shape[1]
    dcc = lw["conv_cc"].shape[1]
    wsc = lw["conv_sc"].shape[0]
    wcc = lw["conv_cc"].shape[0]
    nh, hd = lw["nh"], lw["hd"]
    hsc, hcc = (wsc - 1) * stride, (wcc - 1) * stride
    psc, pcc = (-hsc) % SUBLANES, (-hcc) % SUBLANES
    body = functools.partial(_mix_in_body, tm=tm, stride=stride, da=da, dsc=dsc, dcc=dcc,
                             wsc=wsc, wcc=wcc, nh=nh, hd=hd)
    rowspec = lambda w: pl.BlockSpec((tm, w), lambda g, i: (g * tpg + i, 0))
    featspec = lambda r: pl.BlockSpec((1, r, tm), lambda g, i: (g, 0, i))
    const = lambda a: pl.BlockSpec(a.shape, lambda g, i: (0,) * a.ndim)
    grp = lambda r, c: pl.BlockSpec((1, r, c), lambda g, i: (g, 0, 0))
    consts = [lw["g_mix"], lw["w_qt"], lw["w_kt"], lw["w_vt"], lw["w_ka"], lw["p_k"], lw["k_const"],
              lw["w_fgt"], lw["b_ft"], lw["w_fg"], lw["b_f"], lw["w_cv"],
              lw["conv_sc"], lw["conv_cc"], lw["b_cc"], lw["ln_g_cc"], lw["ln_b_cc"]]
    feat = lambda r, dt: jax.ShapeDtypeStruct((groups, r, rpg), dt)
    return pl.pallas_call(
        body,
        grid=(groups, tpg),
        in_specs=[rowspec(d)] + [const(a) for a in consts] + [grp(hsc, dsc), grp(hcc, dcc)],
        out_specs=[featspec(da), featspec(da), featspec(nh * LANES), rowspec(nh * LANES),
                   featspec(da), featspec(nh), rowspec(dsc), rowspec(dcc),
                   grp(hsc, dsc), grp(hcc, dcc)],
        out_shape=[feat(da, F32), feat(da, F32), feat(nh * LANES, BF),
                   jax.ShapeDtypeStruct((n, nh * LANES), BF), feat(da, BF), feat(nh, F32),
                   jax.ShapeDtypeStruct((n, dsc), BF), jax.ShapeDtypeStruct((n, dcc), BF),
                   jax.ShapeDtypeStruct((groups, hsc, dsc), F32),
                   jax.ShapeDtypeStruct((groups, hcc, dcc), F32)],
        scratch_shapes=[pltpu.VMEM((psc + hsc + tm, dsc), F32),
                        pltpu.VMEM((pcc + hcc + tm, dcc), F32),
                        pltpu.VMEM((tm, dsc), F32),
                        pltpu.VMEM((1, LANES), F32),
                        pltpu.VMEM((nh, 1), F32)],
        compiler_params=_params("arbitrary", "arbitrary"),
        name="mix_in",
    )(x, *consts, prev_sc, prev_cc)


def _mix_out_body(x_ref, o_ref, ysc_ref, ycc_ref, g_ref, wg_ref, pa_ref, ps_ref, pc_ref, wo_ref,
                  y_ref, *, d):
    x = x_ref[...]
    h = _rms(x, g_ref[...]).astype(BF)
    merged = None
    for j, (b_ref, p_ref) in enumerate(((o_ref, pa_ref), (ysc_ref, ps_ref), (ycc_ref, pc_ref))):
        gate = jax.nn.sigmoid(_dot(h, wg_ref[:, j * d:(j + 1) * d]))
        term = gate * _dot(b_ref[...], p_ref[...])
        merged = term if merged is None else merged + term
    y_ref[...] = x + _dot(merged.astype(BF), wo_ref[...])


def _mix_out(x, o_attn, ysc, ycc, lw, *, tm):
    n, d = x.shape
    rowspec = lambda a: pl.BlockSpec((tm, a.shape[1]), lambda i: (i, 0))
    const = lambda a: pl.BlockSpec(a.shape, lambda i: (0,) * a.ndim)
    ws = [lw["g_mix"], lw["w_gates"], lw["p_attn"], lw["p_sc"], lw["p_cc"], lw["w_o"]]
    return pl.pallas_call(
        functools.partial(_mix_out_body, d=d),
        grid=(n // tm,),
        in_specs=[rowspec(x), rowspec(o_attn), rowspec(ysc), rowspec(ycc)] + [const(a) for a in ws],
        out_specs=rowspec(x),
        out_shape=jax.ShapeDtypeStruct((n, d), F32),
        compiler_params=_params("parallel"),
        name="mix_out",
    )(x, o_attn, ysc, ycc, *ws)


def _flash_body(qt_ref, k_ref, vt_ref, o_ref, s_ref, p_ref, *, tq, hd, rc, hps):
    qi = pl.program_id(2)
    nck = tq // rc
    heads = range(hps)
    qts = [qt_ref[0, pl.ds(hh * LANES, LANES), :] for hh in heads]

    def tile(kt, carry, diagonal):
        start = pl.multiple_of(kt * tq, tq)
        for hh in heads:
            s_ref[hh] = _dot(k_ref[0, pl.ds(start, tq), pl.ds(hh * LANES, LANES)], qts[hh])

        def chunk(hh, r):
            blk = s_ref[hh, pl.ds(r * rc, rc), :]
            if diagonal:
                key = lax.broadcasted_iota(jnp.int32, blk.shape, 0) + r * rc
                qry = lax.broadcasted_iota(jnp.int32, blk.shape, 1)
                blk = jnp.where(key <= qry, blk, -jnp.inf)
            return blk

        out = []
        for hh in heads:
            m_prev, l_prev, acc = carry[hh]
            part = None
            for r in range(nck):
                b8 = jnp.max(chunk(hh, r).reshape(rc // SUBLANES, SUBLANES, tq), axis=0)
                part = b8 if part is None else jnp.maximum(part, b8)
            m_new = jnp.maximum(m_prev, jnp.max(part, axis=0, keepdims=True))
            alpha = jnp.exp(m_prev - m_new)
            psum = None
            for r in range(nck):
                p = jnp.exp(chunk(hh, r) - m_new)
                p_ref[hh, pl.ds(r * rc, rc), :] = p.astype(BF)
                p8 = jnp.sum(p.reshape(rc // SUBLANES, SUBLANES, tq), axis=0)
                psum = p8 if psum is None else psum + p8
            l_new = alpha * l_prev + jnp.sum(psum, axis=0, keepdims=True)
            pv = _dot(vt_ref[0, pl.ds(hh * hd, hd), pl.ds(start, tq)], p_ref[hh])
            out.append((m_new, l_new, alpha * acc + pv))
        return tuple(out)

    init = tuple((jnp.full((1, tq), -jnp.inf, F32), jnp.zeros((1, tq), F32),
                  jnp.zeros((hd, tq), F32)) for _ in heads)
    carry = lax.fori_loop(0, qi, lambda kt, cr: tile(kt, cr, False), init)
    fin = tile(qi, carry, True)
    both = jnp.concatenate([acc / l_fin for _, l_fin, acc in fin], axis=0)
    o_ref[0] = both.T.astype(BF)


def _flash_prompt(qta, ka, vtb, *, hd, tq, hps):
    b, t, _ = ka.shape
    da = vtb.shape[1]
    nq = t // tq
    return pl.pallas_call(
        functools.partial(_flash_body, tq=tq, hd=hd, rc=64, hps=hps),
        grid=(b, da // (hps * hd), nq),
        in_specs=[
            pl.BlockSpec((1, hps * LANES, tq), lambda bb, hp, qi: (bb, hp, qi)),
            pl.BlockSpec((1, t, hps * LANES), lambda bb, hp, qi: (bb, 0, hp)),
            pl.BlockSpec((1, hps * hd, t), lambda bb, hp, qi: (bb, hp, 0)),
        ],
        out_specs=pl.BlockSpec((1, tq, hps * hd), lambda bb, hp, qi: (bb, qi, hp)),
        out_shape=jax.ShapeDtypeStruct((b, t, da), BF),
        scratch_shapes=[pltpu.VMEM((hps, tq, tq), F32), pltpu.VMEM((hps, tq, tq), BF)],
        compiler_params=_params("parallel", "parallel", "arbitrary"),
        name="flash_prompt",
    )(qta, ka, vtb)


def _sattn_body(pt_ref, qe_ref, kn_ref, vn_ref, lfn_ref, *rest, npg, nh, hd, page, nq):
    k_refs, v_refs, lf_refs = rest[0:npg], rest[npg:2 * npg], rest[2 * npg:3 * npg]
    o_ref = rest[3 * npg]
    m_ref, l_ref, acc_ref, carry_ref, sq_ref = rest[3 * npg + 1:]
    c = pl.program_id(1)
    rows = nq * nh
    da = nh * hd

    rowi = lax.broadcasted_iota(jnp.int32, (rows, da), 0)
    lanei = lax.broadcasted_iota(jnp.int32, (rows, da), 1)
    head_cols = (lanei // hd) == (rowi % nh)
    qm = jnp.where(head_cols, qe_ref[0], jnp.zeros((rows, da), BF))

    j_i = lax.broadcasted_iota(jnp.int32, (page, page), 0)
    s_i = lax.broadcasted_iota(jnp.int32, (page, page), 1)
    later = jnp.where(j_i > s_i, 1.0, 0.0).astype(BF)
    key_i = lax.broadcasted_iota(jnp.int32, (rows, page), 1)
    qidx = lax.broadcasted_iota(jnp.int32, (rows, page), 0) // nh

    def page_bias(lf, carry):
        hi, mid, lo = _split3(lf)
        suff = _dot(hi, later) + _dot(mid, later) + _dot(lo, later) + carry
        return jnp.concatenate([suff] * nq, axis=0), carry + jnp.sum(lf, axis=1, keepdims=True)

    def attend(s_list, vt_list):
        s = jnp.concatenate(s_list, axis=1) if len(s_list) > 1 else s_list[0]
        m_prev = m_ref[...]
        m_new = jnp.maximum(m_prev, jnp.max(s, axis=-1, keepdims=True))
        alpha = jnp.exp(m_prev - m_new)
        p = jnp.exp(s - m_new)
        l_ref[...] = alpha * l_ref[...] + jnp.sum(p, axis=-1, keepdims=True)
        pv = None
        for j, vt in enumerate(vt_list):
            t = lax.dot_general(p[:, j * page:(j + 1) * page].astype(BF), vt, NT_DIMS,
                                preferred_element_type=F32)
            pv = t if pv is None else pv + t
        acc_ref[...] = alpha * acc_ref[...] + pv
        m_ref[...] = m_new

    @pl.when(c == 0)
    def _():
        m_ref[...] = jnp.full_like(m_ref, -jnp.inf)
        l_ref[...] = jnp.zeros_like(l_ref)
        acc_ref[...] = jnp.zeros_like(acc_ref)
        bias, carry = page_bias(lfn_ref[0], jnp.zeros((nh, 1), F32))
        sq = jnp.sum(jnp.where(key_i == qidx, bias, 0.0), axis=-1, keepdims=True)
        sq_ref[...] = sq
        carry_ref[...] = carry
        s = _dot(qm, kn_ref[0].astype(BF))
        s = jnp.where(key_i <= qidx, s + bias - sq, -jnp.inf)
        attend([s], [vn_ref[0].astype(BF)])

    carry = carry_ref[...]
    sq = sq_ref[...]
    s_list, vt_list = [], []
    for j in range(npg):
        bias, carry = page_bias(lf_refs[j][0, 0], carry)
        s_list.append(_dot(qm, k_refs[j][0, 0].astype(BF)) + bias - sq)
        vt_list.append(v_refs[j][0, 0].astype(BF))
    carry_ref[...] = carry
    attend(s_list, vt_list)

    @pl.when(c == pl.num_programs(1) - 1)
    def _():
        o = jnp.where(head_cols, acc_ref[...] / l_ref[...], 0.0)
        o_ref[0] = jnp.sum(o.reshape(nq, nh, da), axis=1).astype(BF)


def _sample_attn(layer, page_table, qe, knt, vnt, lfnt, cache_kt, cache_vt, cache_lft, *, nh, hd, nq,
                 npg):
    nseq, n_pages = page_table.shape
    page = cache_kt.shape[3]
    da = nh * hd
    n_chunks = n_pages // npg
    rows = nq * nh

    def pg(j):
        return lambda b, c, pt: (layer, pt[b, n_pages - 1 - (c * npg + j)], 0, 0)

    seq3 = lambda r, w: pl.BlockSpec((1, r, w), lambda b, c, pt: (b, 0, 0))
    in_specs = [seq3(rows, da), seq3(da, page), seq3(da, page), seq3(nh, page)]
    in_specs += [pl.BlockSpec((1, 1, da, page), pg(j)) for j in range(npg)]
    in_specs += [pl.BlockSpec((1, 1, da, page), pg(j)) for j in range(npg)]
    in_specs += [pl.BlockSpec((1, 1, nh, page), pg(j)) for j in range(npg)]
    grid_spec = pltpu.PrefetchScalarGridSpec(
        num_scalar_prefetch=1,
        grid=(nseq, n_chunks),
        in_specs=in_specs,
        out_specs=pl.BlockSpec((1, nq, da), lambda b, c, pt: (b, 0, 0)),
        scratch_shapes=[pltpu.VMEM((rows, 1), F32), pltpu.VMEM((rows, 1), F32),
                        pltpu.VMEM((rows, da), F32), pltpu.VMEM((nh, 1), F32),
                        pltpu.VMEM((rows, 1), F32)],
    )
    return pl.pallas_call(
        functools.partial(_sattn_body, npg=npg, nh=nh, hd=hd, page=page, nq=nq),
        grid_spec=grid_spec,
        out_shape=jax.ShapeDtypeStruct((nseq, nq, da), BF),
        compiler_params=_params("arbitrary", "arbitrary"),
        name="sample_attn",
    )(page_table, qe, knt, vnt, lfnt, *([cache_kt] * npg), *([cache_vt] * npg),
      *([cache_lft] * npg))


def _key_placement(nh, hd):
    p = jnp.zeros((3, LANES, nh * LANES), F32)
    for part in range(3):
        for hh in range(nh):
            p = p.at[part, hh, hh * LANES + hd + 3 * SUBLANES + part].set(-1.0)
    return p.astype(BF)


def _key_constant(nh, hd):
    kc = jnp.zeros((1, nh * LANES), F32)
    for part in range(3):
        for hh in range(nh):
            kc = kc.at[0, hh * LANES + hd + part * SUBLANES + hh].set(1.0)
    return kc


def _layer_weights(l, nh, hd, g_ffn1, w1_a, w3_a, w2_a, g_mix, w_in, b_f, conv_sc, conv_cc, b_cc,
                   ln_g_cc, ln_b_cc, p_attn, p_sc, p_cc, w_o, g_ffn2, w1_b, w3_b, w2_b):
    da = nh * hd
    dsc = conv_sc.shape[2]
    dcc = conv_cc.shape[2]
    win = w_in[l]
    o_fg = 3 * da
    o_cv = o_fg + nh
    o_g = o_cv + 3 * dsc + 2 * dcc
    row = lambda a: a[l].reshape(1, -1)
    return dict(
        nh=nh, hd=hd,
        g_ffn1=row(g_ffn1), w1_a=w1_a[l].astype(BF), w3_a=w3_a[l].astype(BF), w2_a=w2_a[l].astype(BF),
        g_ffn2=row(g_ffn2), w1_b=w1_b[l].astype(BF), w3_b=w3_b[l].astype(BF), w2_b=w2_b[l].astype(BF),
        g_mix=row(g_mix),
        w_qt=(win[:, 0:da].T * float(hd) ** -0.5).astype(BF),
        w_kt=win[:, da:2 * da].T.astype(BF),
        w_vt=win[:, 2 * da:3 * da].T.astype(BF),
        w_ka=jnp.pad(win[:, da:2 * da].reshape(-1, nh, hd),
                     ((0, 0), (0, 0), (0, LANES - hd))).reshape(-1, nh * LANES).astype(BF),
        p_k=_key_placement(nh, hd), k_const=_key_constant(nh, hd),
        w_fgt=win[:, o_fg:o_cv].T.astype(BF), b_ft=b_f[l].reshape(nh, 1),
        w_fg=jnp.pad(win[:, o_fg:o_cv], ((0, 0), (0, LANES - nh))).astype(BF),
        b_f=jnp.pad(b_f[l], (0, LANES - nh)).reshape(1, LANES),
        w_cv=win[:, o_cv:o_g].astype(BF),
        w_gates=win[:, o_g:].astype(BF),
        conv_sc=conv_sc[l], conv_cc=conv_cc[l], b_cc=row(b_cc), ln_g_cc=row(ln_g_cc),
        ln_b_cc=row(ln_b_cc),
        p_attn=p_attn[l].astype(BF), p_sc=p_sc[l].astype(BF), p_cc=p_cc[l].astype(BF),
        w_o=w_o[l].astype(BF),
    )


def kernel(x_prompt, x_sample, cache_k, cache_v, cache_logf, state_sc, state_cc, page_table, g_ffn1, w1_a, w3_a, w2_a, g_mix, w_in, b_f, conv_sc, conv_cc, b_cc, ln_g_cc, ln_b_cc, p_attn, p_sc, p_cc, w_o, g_ffn2, w1_b, w3_b, w2_b, g_final):
    bp, t, d = x_prompt.shape
    bs, ts, _ = x_sample.shape
    depth = w_in.shape[0]
    nh = b_f.shape[1]
    da = p_attn.shape[1]
    hd = da // nh
    dsc, dcc = conv_sc.shape[2], conv_cc.shape[2]
    wsc, wcc = conv_sc.shape[1], conv_cc.shape[1]
    n_pool, page = cache_k.shape[1], cache_k.shape[2]
    assert LANES % hd == 0 and da % LANES == 0 and bs % SUBLANES == 0 and ts <= page
    assert nh == SUBLANES and page == LANES and hd + 4 * SUBLANES <= LANES

    tm_p = _pick(t, (512, 256, 128))
    tq = _pick(t, (512, 256, 128))
    hps = 2 * LANES // hd
    npg = _pick(page_table.shape[1], (8, 4, 2, 1))
    ns = bs * ts
    gfin = g_final.reshape(1, d)

    ckt = jnp.transpose(cache_k, (0, 1, 3, 4, 2)).reshape(depth, n_pool, da, page)
    cvt = jnp.transpose(cache_v, (0, 1, 3, 4, 2)).reshape(depth, n_pool, da, page)
    clft = jnp.swapaxes(cache_logf, 2, 3)

    yp = x_prompt.reshape(bp * t, d)
    ys = jnp.swapaxes(x_sample, 0, 1).reshape(ns, d)
    zero_sc = jnp.zeros((bp, wsc - 1, dsc), F32)
    zero_cc = jnp.zeros((bp, wcc - 1, dcc), F32)

    outs = [[] for _ in range(10)]
    for l in range(depth):
        lw = _layer_weights(l, nh, hd, g_ffn1, w1_a, w3_a, w2_a, g_mix, w_in, b_f, conv_sc, conv_cc,
                            b_cc, ln_g_cc, ln_b_cc, p_attn, p_sc, p_cc, w_o, g_ffn2, w1_b, w3_b, w2_b)
        last = l == depth - 1

        yp = _ffn(yp, lw["g_ffn1"], lw["w1_a"], lw["w3_a"], lw["w2_a"], gfin, final=False, tm=tm_p)
        ktf, vtf, qta, ka, vtb, lft, ysc, ycc, ssc, scc = _mix_in(
            yp, lw, zero_sc, zero_cc, groups=bp, tm=tm_p, stride=1)
        o = _flash_prompt(qta, ka.reshape(bp, t, nh * LANES), vtb, hd=hd, tq=tq, hps=hps)
        yp = _mix_out(yp, o.reshape(bp * t, da), ysc, ycc, lw, tm=tm_p)
        yp = _ffn(yp, lw["g_ffn2"], lw["w1_b"], lw["w3_b"], lw["w2_b"], gfin, final=last, tm=tm_p)
        time_major = lambda a: jnp.transpose(a.reshape(bp, nh, hd, t), (0, 3, 1, 2))
        outs[0].append(time_major(ktf))
        outs[1].append(time_major(vtf))
        outs[2].append(jnp.swapaxes(lft, 1, 2))
        outs[3].append(ssc)
        outs[4].append(scc)

        ys = _ffn(ys, lw["g_ffn1"], lw["w1_a"], lw["w3_a"], lw["w2_a"], gfin, final=False, tm=ns)
        psc = jnp.swapaxes(state_sc[l], 0, 1).reshape(1, (wsc - 1) * bs, dsc)
        pcc = jnp.swapaxes(state_cc[l], 0, 1).reshape(1, (wcc - 1) * bs, dcc)
        ktf, vtf, qta, ka, vtb, lft, ysc, ycc, ssc, scc = _mix_in(
            ys, lw, psc, pcc, groups=1, tm=ns, stride=bs)
        per_seq = lambda a: jnp.transpose(a.reshape(-1, ts, bs), (2, 0, 1))
        featmaj = lambda a: jnp.pad(per_seq(a), ((0, 0), (0, 0), (0, page - ts)))
        q_rows = jnp.swapaxes(per_seq(qta.reshape(nh, LANES, ns)[:, 0:hd, :]), 1, 2)
        qe = jnp.repeat(q_rows, nh, axis=1)
        o = _sample_attn(l, page_table, qe, featmaj(ktf), featmaj(vtf), featmaj(lft), ckt, cvt, clft,
                         nh=nh, hd=hd, nq=ts, npg=npg)
        o = jnp.swapaxes(o, 0, 1).reshape(ns, da)
        ys = _mix_out(ys, o, ysc, ycc, lw, tm=ns)
        ys = _ffn(ys, lw["g_ffn2"], lw["w1_b"], lw["w3_b"], lw["w2_b"], gfin, final=last, tm=ns)
        seq_rows = lambda a: jnp.swapaxes(per_seq(a), 1, 2)
        outs[5].append(seq_rows(ktf).reshape(bs, ts, nh, hd))
        outs[6].append(seq_rows(vtf).reshape(bs, ts, nh, hd))
        outs[7].append(seq_rows(lft))
        outs[8].append(jnp.swapaxes(ssc.reshape(wsc - 1, bs, dsc), 0, 1))
        outs[9].append(jnp.swapaxes(scc.reshape(wcc - 1, bs, dcc), 0, 1))

    y_prompt = yp.reshape(bp, t, d)
    y_sample = jnp.swapaxes(ys.reshape(ts, bs, d), 0, 1)
    return (y_prompt, y_sample) + tuple(jnp.stack(o) for o in outs)
```
